```python
import jax, jax.numpy as jnp
from jax import lax
import numpy as np

D_MODEL = 1024
BATCH = 8
SEQ = 2048
DEPTH = 4

GLA_HEADS = 4
GLA_DK = 64
GLA_DV = 128
GLA_KEY = GLA_HEADS * GLA_DK
GLA_VAL = GLA_HEADS * GLA_DV
GLA_RANK = 16
GLA_TAU = 16.0
GLA_CHUNK = 64
SGU_HEADS = 4
SGU_CHUNK = 128
SGU_W = 512
SGU_DH = SGU_W // SGU_HEADS
CONV_GROUPS = 4
CONV_W = 512
CONV_K = 3
N_BRANCH = 3
D_FF = 2816
FFN_CONV_K = 3
EPS = 1e-6

IN_SPLITS = (GLA_KEY, GLA_KEY, GLA_VAL, GLA_VAL, GLA_RANK, SGU_W, SGU_W, CONV_W, CONV_W, CONV_W, N_BRANCH * D_MODEL)
IN_COLS = 2 * GLA_KEY + 2 * GLA_VAL + GLA_RANK + 2 * SGU_W + 3 * CONV_W + N_BRANCH * D_MODEL

kernel_name = 'hybrid_gla_sgu_shortconv_convffn'


def rms_norm(x, g):
    xf = x.astype(jnp.float32)
    y = xf * lax.rsqrt(jnp.mean(xf * xf, axis=-1, keepdims=True) + EPS)
    return (y * g.astype(jnp.float32)).astype(x.dtype)


def layer_norm(x, g, b):
    xf = x.astype(jnp.float32)
    mu = jnp.mean(xf, axis=-1, keepdims=True)
    xc = xf - mu
    y = xc * lax.rsqrt(jnp.mean(xc * xc, axis=-1, keepdims=True) + EPS)
    return (y * g.astype(jnp.float32) + b.astype(jnp.float32)).astype(x.dtype)


def causal_dwconv(x, w):
    K = w.shape[0]
    S = x.shape[1]
    xp = jnp.pad(x, ((0, 0), (K - 1, 0), (0, 0)))
    y = xp[:, 0:S] * w[0]
    for kk in range(1, K):
        y = y + xp[:, kk:kk + S] * w[kk]
    return y


def gla_chunked(q, k, v, log_a):
    Bsz, S, H, dk = q.shape
    dv = v.shape[-1]
    C = GLA_CHUNK
    n = S // C
    f32 = jnp.float32
    q, k, v, log_a = (t.astype(f32).reshape(Bsz, n, C, H, t.shape[-1]) for t in (q, k, v, log_a))
    b = jnp.cumsum(log_a, axis=2)
    b_last = b[:, :, -1:]
    q_t = q * jnp.exp(b)
    k_t = k * jnp.exp(-b)
    k_end = k * jnp.exp(b_last - b)
    causal = jnp.tril(jnp.ones((C, C), dtype=bool))
    scores = jnp.einsum('bnihk,bnjhk->bnhij', q_t, k_t)
    scores = jnp.where(causal, scores, 0.0)
    o_intra = jnp.einsum('bnhij,bnjhv->bnihv', scores, v)
    chunk_state = jnp.einsum('bnjhk,bnjhv->bnhkv', k_end, v)
    decay = jnp.exp(b_last[:, :, 0])

    def step(s_prev, inp):
        dec, cs = inp
        return dec[..., None] * s_prev + cs, s_prev

    s0 = jnp.zeros((Bsz, H, dk, dv), f32)
    _, s_prevs = lax.scan(step, s0, (jnp.moveaxis(decay, 1, 0), jnp.moveaxis(chunk_state, 1, 0)))
    s_prevs = jnp.moveaxis(s_prevs, 0, 1)
    o_inter = jnp.einsum('bnihk,bnhkv->bnihv', q_t, s_prevs)
    return (o_intra + o_inter).reshape(Bsz, S, H, dv)


def sgu_chunked(u, v, ln_g, ln_b, ws, bs):
    Bsz, S, _ = v.shape
    n = S // SGU_CHUNK
    v = layer_norm(v, ln_g, ln_b)
    v = v.reshape(Bsz, n, SGU_CHUNK, SGU_HEADS, SGU_DH)
    ws_causal = jnp.where(jnp.tril(jnp.ones((SGU_CHUNK, SGU_CHUNK), dtype=bool)), ws, 0.0)
    mixed = jnp.einsum('hts,bnshd->bnthd', ws_causal, v) + jnp.transpose(bs)[None, None, :, :, None]
    return u * mixed.reshape(Bsz, S, SGU_W)


def setup_inputs(seed: int = 0) -> dict:
    key = jax.random.key(seed)
    ks = jax.random.split(key, 24)
    f32 = jnp.float32

    def nrm(k, shape, scale):
        return jax.random.normal(k, shape, f32) * scale

    L = DEPTH
    return {
        'x': nrm(ks[0], (BATCH, SEQ, D_MODEL), 1.0),
        'norm_mix': 1.0 + nrm(ks[1], (L, D_MODEL), 0.02),
        'w_in': nrm(ks[2], (L, D_MODEL, IN_COLS), D_MODEL ** -0.5),
        'gla_w_alpha': nrm(ks[3], (L, GLA_RANK, GLA_KEY), GLA_RANK ** -0.5),
        'gla_b_alpha': nrm(ks[4], (L, GLA_KEY), 0.1),
        'gla_norm': 1.0 + nrm(ks[5], (L, GLA_VAL), 0.02),
        'gla_w_out': nrm(ks[6], (L, GLA_VAL, D_MODEL), GLA_VAL ** -0.5),
        'sgu_ln_g': 1.0 + nrm(ks[7], (L, SGU_W), 0.02),
        'sgu_ln_b': nrm(ks[8], (L, SGU_W), 0.02),
        'sgu_ws': nrm(ks[9], (L, SGU_HEADS, SGU_CHUNK, SGU_CHUNK), SGU_CHUNK ** -0.5),
        'sgu_bs': 1.0 + nrm(ks[10], (L, SGU_HEADS, SGU_CHUNK), 0.02),
        'sgu_w_out': nrm(ks[11], (L, SGU_W, D_MODEL), SGU_W ** -0.5),
        'conv_w': nrm(ks[12], (L, CONV_K, CONV_W), CONV_K ** -0.5),
        'conv_w_out': nrm(ks[13], (L, CONV_W, D_MODEL), CONV_W ** -0.5),
        'w_o': nrm(ks[14], (L, D_MODEL, D_MODEL), D_MODEL ** -0.5),
        'norm_ffn': 1.0 + nrm(ks[15], (L, D_MODEL), 0.02),
        'ffn_w_up': nrm(ks[16], (L, D_MODEL, 2 * D_FF), D_MODEL ** -0.5),
        'ffn_conv_w': nrm(ks[17], (L, FFN_CONV_K, 2 * D_FF), FFN_CONV_K ** -0.5),
        'ffn_conv_b': nrm(ks[18], (L, 2 * D_FF), 0.01),
        'ffn_w_down': nrm(ks[19], (L, D_FF, D_MODEL), D_FF ** -0.5),
        'norm_final': 1.0 + nrm(ks[20], (D_MODEL,), 0.02),
    }


def reference(x, norm_mix, w_in, gla_w_alpha, gla_b_alpha, gla_norm, gla_w_out,
              sgu_ln_g, sgu_ln_b, sgu_ws, sgu_bs, sgu_w_out, conv_w, conv_w_out,
              w_o, norm_ffn, ffn_w_up, ffn_conv_w, ffn_conv_b, ffn_w_down, norm_final):
    Bsz, S, D = x.shape
    split_points = np.cumsum(IN_SPLITS)[:-1].tolist()
    for l in range(DEPTH):
        h = rms_norm(x, norm_mix[l])
        proj = h @ w_in[l]
        (q, k, v, g_out, a_lr, su, sv, cb, cc, cx, gate_logits) = jnp.split(proj, split_points, axis=-1)

        log_a = jax.nn.log_sigmoid((a_lr @ gla_w_alpha[l] + gla_b_alpha[l]).astype(jnp.float32)) / GLA_TAU
        o = gla_chunked(q.reshape(Bsz, S, GLA_HEADS, GLA_DK) * (GLA_DK ** -0.5),
                        k.reshape(Bsz, S, GLA_HEADS, GLA_DK),
                        v.reshape(Bsz, S, GLA_HEADS, GLA_DV),
                        log_a.reshape(Bsz, S, GLA_HEADS, GLA_DK))
        o = rms_norm(o, gla_norm[l].reshape(GLA_HEADS, GLA_DV)).astype(x.dtype).reshape(Bsz, S, GLA_VAL)
        y_a = (o * jax.nn.silu(g_out)) @ gla_w_out[l]

        y_b = sgu_chunked(jax.nn.gelu(su), jax.nn.gelu(sv), sgu_ln_g[l], sgu_ln_b[l],
                          sgu_ws[l], sgu_bs[l]) @ sgu_w_out[l]

        y_c = (cb * causal_dwconv(cc * cx, conv_w[l])) @ conv_w_out[l]

        gt = jax.nn.sigmoid(gate_logits).reshape(Bsz, S, N_BRANCH, D)
        merged = gt[:, :, 0] * y_a + gt[:, :, 1] * y_b + gt[:, :, 2] * y_c
        x = x + merged @ w_o[l]

        h = rms_norm(x, norm_ffn[l])
        a = causal_dwconv(h @ ffn_w_up[l], ffn_conv_w[l]) + ffn_conv_b[l]
        gate, val = jnp.split(a, 2, axis=-1)
        x = x + (jax.nn.silu(gate) * val) @ ffn_w_down[l]
    return rms_norm(x, norm_final)
```

```python
import functools

import jax
import jax.numpy as jnp
from jax import lax
from jax.experimental import pallas as pl
from jax.experimental.pallas import tpu as pltpu

F32 = jnp.float32
BF16 = jnp.bfloat16

D_MODEL = 1024
DEPTH = 4
GLA_HEADS = 4
GLA_DK = 64
GLA_DV = 128
GLA_KEY = GLA_HEADS * GLA_DK
GLA_VAL = GLA_HEADS * GLA_DV
GLA_RANK = 16
GLA_TAU = 16.0
GLA_CHUNK = 64
SGU_HEADS = 4
SGU_CHUNK = 128
SGU_W = 512
SGU_DH = SGU_W // SGU_HEADS
CONV_W = 512
N_BRANCH = 3
D_FF = 2816
EPS = 1e-6

LANES = 128
SUBLANES = 8
VMEM_LIMIT_BYTES = 56 * 1024 * 1024

RANK_PAD = LANES
COL_Q = 0
COL_K = COL_Q + GLA_KEY
COL_V = COL_K + GLA_KEY
COL_G = COL_V + GLA_VAL
COL_ALR = COL_G + GLA_VAL
COL_SGU = COL_ALR + RANK_PAD
COL_CONV = COL_SGU + 2 * SGU_W
COL_GATE = COL_CONV + 3 * CONV_W
IN_COLS_PACKED = COL_GATE + N_BRANCH * D_MODEL

MIXER_TM = 256
FFN_TM = 256
FFN_BLK = 256
CUMSUM_BLK = 256


def _dot(a, b):
    return jnp.dot(a, b, preferred_element_type=F32)


def _dot_tb(a, b):
    return lax.dot_general(a, b, (((1,), (1,)), ((), ())), preferred_element_type=F32)


def _dot_ta(a, b):
    return lax.dot_general(a, b, (((0,), (0,)), ((), ())), preferred_element_type=F32)


def _rms_norm(x, g):
    ms = jnp.mean(x * x, axis=-1, keepdims=True)
    return x * lax.rsqrt(ms + EPS) * g


def _silu(x):
    return x * jax.nn.sigmoid(x)


def _causal_conv3(x, prev, w, rows):
    x1 = jnp.where(rows == 0, prev[7:8], pltpu.roll(x, 1, 0))
    x2 = jnp.where(rows == 0, prev[6:7], jnp.where(rows == 1, prev[7:8], pltpu.roll(x, 2, 0)))
    return x2 * w[0:1] + x1 * w[1:2] + x * w[2:3]


def _chunk_cumsum(la):
    t = la.shape[0]
    blk = min(t, CUMSUM_BLK)
    r = lax.broadcasted_iota(jnp.int32, (blk, blk), 0)
    c = lax.broadcasted_iota(jnp.int32, (blk, blk), 1)
    shift = GLA_CHUNK.bit_length() - 1
    tri = jnp.where(((r >> shift) == (c >> shift)) & (c <= r), 1.0, 0.0).astype(BF16)
    hi = la.astype(BF16)
    lo = (la - hi.astype(F32)).astype(BF16)
    parts = []
    for i in range(t // blk):
        sl = slice(i * blk, (i + 1) * blk)
        parts.append(_dot(tri, hi[sl]) + _dot(tri, lo[sl]))
    return parts[0] if len(parts) == 1 else jnp.concatenate(parts, axis=0)


def _gla_branch(proj, walpha_ref, balpha_ref, gnorm_ref, wa_ref, st_ref, tm):
    pa = proj(COL_Q, COL_SGU)
    q = pa[:, COL_Q:COL_K] * (GLA_DK ** -0.5)
    k = pa[:, COL_K:COL_V]
    v = pa[:, COL_V:COL_G]
    g = pa[:, COL_G:COL_ALR]
    alr = pa[:, COL_ALR:COL_SGU].astype(BF16)
    z = _dot(alr, walpha_ref[...]) + balpha_ref[...]
    log_a = (jnp.minimum(z, 0.0) - jnp.log1p(jnp.exp(-jnp.abs(z)))) * (1.0 / GLA_TAU)
    b = _chunk_cumsum(log_a)

    c_rows = GLA_HEADS * GLA_CHUNK
    dk_shift = GLA_DK.bit_length() - 1
    lane_head = lax.broadcasted_iota(jnp.int32, (GLA_CHUNK, GLA_KEY), 1) >> dk_shift
    prow = lax.broadcasted_iota(jnp.int32, (c_rows, GLA_CHUNK), 0) & (GLA_CHUNK - 1)
    pcol = lax.broadcasted_iota(jnp.int32, (c_rows, GLA_CHUNK), 1)
    causal = pcol <= prow

    outs = []
    for c in range(tm // GLA_CHUNK):
        sl = slice(c * GLA_CHUNK, (c + 1) * GLA_CHUNK)
        bc = b[sl]
        b_last = bc[GLA_CHUNK - 1:GLA_CHUNK]
        qt = q[sl] * jnp.exp(bc)
        kt = (k[sl] * jnp.exp(-bc)).astype(BF16)
        kend = (k[sl] * jnp.exp(b_last - bc)).astype(BF16)
        vc = v[sl].astype(BF16)
        dec = jnp.exp(b_last)
        qstack = jnp.concatenate(
            [jnp.where(lane_head == hh, qt, 0.0) for hh in range(GLA_HEADS)], axis=0).astype(BF16)
        p = jnp.where(causal, _dot_tb(qstack, kt), 0.0).astype(BF16)
        st = st_ref[...]
        o_big = _dot(p, vc) + _dot_tb(qstack, st.astype(BF16))
        st_ref[...] = st * dec + _dot_ta(vc, kend)
        outs.append(jnp.concatenate(
            [o_big[hh * GLA_CHUNK:(hh + 1) * GLA_CHUNK, hh * GLA_DV:(hh + 1) * GLA_DV]
             for hh in range(GLA_HEADS)], axis=1))
    o = jnp.concatenate(outs, axis=0)

    gn = gnorm_ref[...]
    normed = []
    for hh in range(GLA_HEADS):
        hs = slice(hh * GLA_DV, (hh + 1) * GLA_DV)
        normed.append(_rms_norm(o[:, hs], gn[:, hs]))
    on = jnp.concatenate(normed, axis=1)
    return _dot((on * _silu(g)).astype(BF16), wa_ref[...])


def _sgu_branch(proj, lng_ref, lnb_ref, ws_ref, sbias_ref, wb_ref, tm):
    pb = proj(COL_SGU, COL_CONV)
    u = jax.nn.gelu(pb[:, :SGU_W])
    sv = jax.nn.gelu(pb[:, SGU_W:])
    mu = jnp.mean(sv, axis=-1, keepdims=True)
    xc = sv - mu
    y = xc * lax.rsqrt(jnp.mean(xc * xc, axis=-1, keepdims=True) + EPS)
    vn = (y * lng_ref[...] + lnb_ref[...]).astype(BF16)

    r = lax.broadcasted_iota(jnp.int32, (SGU_CHUNK, SGU_CHUNK), 0)
    c = lax.broadcasted_iota(jnp.int32, (SGU_CHUNK, SGU_CHUNK), 1)
    ws = [jnp.where(c <= r, ws_ref[hh], 0.0).astype(BF16) for hh in range(SGU_HEADS)]
    sbias = sbias_ref[...]
    chunks = []
    for cc in range(tm // SGU_CHUNK):
        rs = slice(cc * SGU_CHUNK, (cc + 1) * SGU_CHUNK)
        mixed = jnp.concatenate(
            [_dot(ws[hh], vn[rs, hh * SGU_DH:(hh + 1) * SGU_DH]) for hh in range(SGU_HEADS)], axis=1)
        chunks.append(mixed + sbias)
    mixed = chunks[0] if len(chunks) == 1 else jnp.concatenate(chunks, axis=0)
    return _dot((u * mixed).astype(BF16), wb_ref[...])


def _conv_branch(proj, cw_ref, wc_ref, halo_ref, tm):
    pc = proj(COL_CONV, COL_GATE)
    cb = pc[:, :CONV_W]
    xin = pc[:, CONV_W:2 * CONV_W] * pc[:, 2 * CONV_W:]
    rows = lax.broadcasted_iota(jnp.int32, (tm, CONV_W), 0)
    y = _causal_conv3(xin, halo_ref[...], cw_ref[...], rows)
    halo_ref[...] = xin[tm - SUBLANES:]
    return _dot((cb * y).astype(BF16), wc_ref[...])


def _mixer_kernel(x_ref, nrm_ref, win_ref, walpha_ref, balpha_ref, gnorm_ref, wa_ref,
                  lng_ref, lnb_ref, ws_ref, sbias_ref, wb_ref, cw_ref, wc_ref, wo_ref,
                  o_ref, st_ref, halo_ref, *, tm):
    @pl.when(pl.program_id(1) == 0)
    def _():
        st_ref[...] = jnp.zeros_like(st_ref)
        halo_ref[...] = jnp.zeros_like(halo_ref)

    x = x_ref[...]
    h = _rms_norm(x, nrm_ref[...]).astype(BF16)

    def proj(lo, hi):
        return _dot(h, win_ref[:, lo:hi])

    def gate(i):
        return jax.nn.sigmoid(proj(COL_GATE + i * D_MODEL, COL_GATE + (i + 1) * D_MODEL))

    merged = gate(0) * _gla_branch(proj, walpha_ref, balpha_ref, gnorm_ref, wa_ref, st_ref, tm)
    merged = merged + gate(1) * _sgu_branch(proj, lng_ref, lnb_ref, ws_ref, sbias_ref, wb_ref, tm)
    merged = merged + gate(2) * _conv_branch(proj, cw_ref, wc_ref, halo_ref, tm)
    o_ref[...] = x + _dot(merged.astype(BF16), wo_ref[...])


def _ffn_kernel(x_ref, nrm_ref, wup_ref, cw_ref, cb_ref, wdn_ref, nf_ref,
                o_ref, act_ref, halo_ref, *, tm, final):
    @pl.when(pl.program_id(1) == 0)
    def _():
        halo_ref[...] = jnp.zeros_like(halo_ref)

    x = x_ref[...]
    h = _rms_norm(x, nrm_ref[...]).astype(BF16)
    rows = lax.broadcasted_iota(jnp.int32, (tm, FFN_BLK), 0)

    def conv_cols(lo):
        cs = slice(lo, lo + FFN_BLK)
        a = _dot(h, wup_ref[:, cs])
        y = _causal_conv3(a, halo_ref[:, cs], cw_ref[:, cs], rows) + cb_ref[:, cs]
        halo_ref[:, cs] = a[tm - SUBLANES:]
        return y

    for blk in range(D_FF // FFN_BLK):
        gate = conv_cols(blk * FFN_BLK)
        val = conv_cols(D_FF + blk * FFN_BLK)
        act_ref[:, blk * FFN_BLK:(blk + 1) * FFN_BLK] = (_silu(gate) * val).astype(BF16)

    out = x + _dot(act_ref[...], wdn_ref[...])
    if final:
        out = _rms_norm(out, nf_ref[...])
    o_ref[...] = out


def _const_spec(shape):
    zeros = (0,) * len(shape)
    return pl.BlockSpec(shape, lambda b, j: zeros, pipeline_mode=pl.Buffered(1))


def _tile_spec(tm):
    return pl.BlockSpec((None, tm, D_MODEL), lambda b, j: (b, j, 0))


def _compiler_params():
    return pltpu.CompilerParams(dimension_semantics=("arbitrary", "arbitrary"),
                                vmem_limit_bytes=VMEM_LIMIT_BYTES)


def _mixer_layer(x, weights):
    bsz, seq, _ = x.shape
    tm = MIXER_TM
    return pl.pallas_call(
        functools.partial(_mixer_kernel, tm=tm),
        grid=(bsz, seq // tm),
        in_specs=[_tile_spec(tm)] + [_const_spec(w.shape) for w in weights],
        out_specs=_tile_spec(tm),
        out_shape=jax.ShapeDtypeStruct(x.shape, F32),
        scratch_shapes=[pltpu.VMEM((GLA_VAL, GLA_KEY), F32),
                        pltpu.VMEM((SUBLANES, CONV_W), F32)],
        compiler_params=_compiler_params(),
        name="mixer",
    )(x, *weights)


def _ffn_layer(x, weights, final):
    bsz, seq, _ = x.shape
    tm = FFN_TM
    return pl.pallas_call(
        functools.partial(_ffn_kernel, tm=tm, final=final),
        grid=(bsz, seq // tm),
        in_specs=[_tile_spec(tm)] + [_const_spec(w.shape) for w in weights],
        out_specs=_tile_spec(tm),
        out_shape=jax.ShapeDtypeStruct(x.shape, F32),
        scratch_shapes=[pltpu.VMEM((tm, D_FF), BF16),
                        pltpu.VMEM((SUBLANES, 2 * D_FF), F32)],
        compiler_params=_compiler_params(),
        name="ffn_final" if final else "ffn",
    )(x, *weights)


def _row(v):
    return v.reshape(1, -1).astype(F32)


def _pack_w_in(w):
    pad = jnp.zeros((D_MODEL, RANK_PAD - GLA_RANK), w.dtype)
    cut = 2 * GLA_KEY + 2 * GLA_VAL
    return jnp.concatenate([w[:, :cut], w[:, cut:cut + GLA_RANK], pad, w[:, cut + GLA_RANK:]],
                           axis=1).astype(BF16)


def kernel(x, norm_mix, w_in, gla_w_alpha, gla_b_alpha, gla_norm, gla_w_out, sgu_ln_g, sgu_ln_b, sgu_ws, sgu_bs, sgu_w_out, conv_w, conv_w_out, w_o, norm_ffn, ffn_w_up, ffn_conv_w, ffn_conv_b, ffn_w_down, norm_final):
    assert x.shape[1] % MIXER_TM == 0 and x.shape[1] % FFN_TM == 0
    assert MIXER_TM % SGU_CHUNK == 0 and MIXER_TM % CUMSUM_BLK == 0 and D_FF % FFN_BLK == 0
    x = x.astype(F32)
    for l in range(DEPTH):
        walpha = jnp.concatenate(
            [gla_w_alpha[l], jnp.zeros((RANK_PAD - GLA_RANK, GLA_KEY), gla_w_alpha.dtype)], axis=0)
        sbias = jnp.repeat(jnp.transpose(sgu_bs[l]), SGU_DH, axis=1)
        mixer_weights = (
            _row(norm_mix[l]), _pack_w_in(w_in[l]), walpha.astype(BF16), _row(gla_b_alpha[l]),
            _row(gla_norm[l]), gla_w_out[l].astype(BF16),
            _row(sgu_ln_g[l]), _row(sgu_ln_b[l]), sgu_ws[l].astype(F32), sbias.astype(F32),
            sgu_w_out[l].astype(BF16),
            conv_w[l].astype(F32), conv_w_out[l].astype(BF16), w_o[l].astype(BF16))
        x = _mixer_layer(x, mixer_weights)
        ffn_weights = (
            _row(norm_ffn[l]), ffn_w_up[l].astype(BF16), ffn_conv_w[l].astype(F32),
            _row(ffn_conv_b[l]), ffn_w_down[l].astype(BF16), _row(norm_final))
        x = _ffn_layer(x, ffn_weights, final=(l == DEPTH - 1))
    return x
```

```python
import functools

import jax
import jax.numpy as jnp
from jax import lax
from jax.experimental import pallas as pl
from jax.experimental.pallas import tpu as pltpu

F32 = jnp.float32
BF16 = jnp.bfloat16

D_MODEL = 1024
DEPTH = 4
GLA_HEADS = 4
GLA_DK = 64
GLA_DV = 128
GLA_KEY = GLA_HEADS * GLA_DK
GLA_VAL = GLA_HEADS * GLA_DV
GLA_RANK = 16
GLA_TAU = 16.0
GLA_CHUNK = 64
SGU_HEADS = 4
SGU_CHUNK = 128
SGU_W = 512
SGU_DH = SGU_W // SGU_HEADS
CONV_W = 512
N_BRANCH = 3
D_FF = 2816
EPS = 1e-6

LANES = 128
SUBLANES = 8
VMEM_LIMIT_BYTES = 56 * 1024 * 1024

RANK_PAD = LANES
HEAD_COLS = 2 * GLA_KEY + 2 * GLA_VAL
COL_Q = 0
COL_K = COL_Q + GLA_KEY
COL_V = COL_K + GLA_KEY
COL_G = COL_V + GLA_VAL
TAIL_SGU = 0
TAIL_CONV = TAIL_SGU + 2 * SGU_W
TAIL_GATE = TAIL_CONV + 3 * CONV_W

MIXER_TM = 512
FFN_TM = 512
FFN_BLK = 256
CUMSUM_BLK = 256


def _dot(a, b):
    return jnp.dot(a, b, preferred_element_type=F32)


def _dot_tb(a, b):
    return lax.dot_general(a, b, (((1,), (1,)), ((), ())), preferred_element_type=F32)


def _dot_ta(a, b):
    return lax.dot_general(a, b, (((0,), (0,)), ((), ())), preferred_element_type=F32)


def _rms_norm(x, g):
    ms = jnp.mean(x * x, axis=-1, keepdims=True)
    return x * lax.rsqrt(ms + EPS) * g


def _silu(x):
    return x * jax.nn.sigmoid(x)


def _causal_conv3(x, prev, w, rows):
    x1 = jnp.where(rows == 0, prev[7:8], pltpu.roll(x, 1, 0))
    x2 = jnp.where(rows == 0, prev[6:7], jnp.where(rows == 1, prev[7:8], pltpu.roll(x, 2, 0)))
    return x2 * w[0:1] + x1 * w[1:2] + x * w[2:3]


def _chunk_cumsum(la):
    t = la.shape[0]
    blk = min(t, CUMSUM_BLK)
    r = lax.broadcasted_iota(jnp.int32, (blk, blk), 0)
    c = lax.broadcasted_iota(jnp.int32, (blk, blk), 1)
    shift = GLA_CHUNK.bit_length() - 1
    tri = jnp.where(((r >> shift) == (c >> shift)) & (c <= r), 1.0, 0.0).astype(BF16)
    hi = la.astype(BF16)
    lo = (la - hi.astype(F32)).astype(BF16)
    parts = []
    for i in range(t // blk):
        sl = slice(i * blk, (i + 1) * blk)
        parts.append(_dot(tri, hi[sl]) + _dot(tri, lo[sl]))
    return parts[0] if len(parts) == 1 else jnp.concatenate(parts, axis=0)


def _gla_branch(h, whead_ref, walr_ref, walpha_ref, balpha_ref, gnorm_ref, wa_ref, st_ref, tm):
    pa = _dot(h, whead_ref[...])
    q = pa[:, COL_Q:COL_K] * (GLA_DK ** -0.5)
    k = pa[:, COL_K:COL_V]
    v = pa[:, COL_V:COL_G]
    g = pa[:, COL_G:HEAD_COLS]
    alr = _dot(h, walr_ref[...]).astype(BF16)
    z = _dot(alr, walpha_ref[...]) + balpha_ref[...]
    log_a = (jnp.minimum(z, 0.0) - jnp.log1p(jnp.exp(-jnp.abs(z)))) * (1.0 / GLA_TAU)
    b = _chunk_cumsum(log_a)

    c_rows = GLA_HEADS * GLA_CHUNK
    dk_shift = GLA_DK.bit_length() - 1
    lane_head = lax.broadcasted_iota(jnp.int32, (GLA_CHUNK, GLA_KEY), 1) >> dk_shift
    prow = lax.broadcasted_iota(jnp.int32, (c_rows, GLA_CHUNK), 0) & (GLA_CHUNK - 1)
    pcol = lax.broadcasted_iota(jnp.int32, (c_rows, GLA_CHUNK), 1)
    causal = pcol <= prow

    outs = []
    for c in range(tm // GLA_CHUNK):
        sl = slice(c * GLA_CHUNK, (c + 1) * GLA_CHUNK)
        bc = b[sl]
        b_last = bc[GLA_CHUNK - 1:GLA_CHUNK]
        qt = q[sl] * jnp.exp(bc)
        kt = (k[sl] * jnp.exp(-bc)).astype(BF16)
        kend = (k[sl] * jnp.exp(b_last - bc)).astype(BF16)
        vc = v[sl].astype(BF16)
        dec = jnp.exp(b_last)
        qstack = jnp.concatenate(
            [jnp.where(lane_head == hh, qt, 0.0) for hh in range(GLA_HEADS)], axis=0).astype(BF16)
        p = jnp.where(causal, _dot_tb(qstack, kt), 0.0).astype(BF16)
        st = st_ref[...]
        o_big = _dot(p, vc) + _dot_tb(qstack, st.astype(BF16))
        st_ref[...] = st * dec + _dot_ta(vc, kend)
        outs.append(jnp.concatenate(
            [o_big[hh * GLA_CHUNK:(hh + 1) * GLA_CHUNK, hh * GLA_DV:(hh + 1) * GLA_DV]
             for hh in range(GLA_HEADS)], axis=1))
    o = jnp.concatenate(outs, axis=0)

    gn = gnorm_ref[...]
    normed = []
    for hh in range(GLA_HEADS):
        hs = slice(hh * GLA_DV, (hh + 1) * GLA_DV)
        normed.append(_rms_norm(o[:, hs], gn[:, hs]))
    on = jnp.concatenate(normed, axis=1)
    return _dot((on * _silu(g)).astype(BF16), wa_ref[...])


def _sgu_branch(tail, lng_ref, lnb_ref, ws_ref, sbias_ref, wb_ref, tm):
    pb = tail(TAIL_SGU, TAIL_CONV)
    u = jax.nn.gelu(pb[:, :SGU_W])
    sv = jax.nn.gelu(pb[:, SGU_W:])
    mu = jnp.mean(sv, axis=-1, keepdims=True)
    xc = sv - mu
    y = xc * lax.rsqrt(jnp.mean(xc * xc, axis=-1, keepdims=True) + EPS)
    vn = (y * lng_ref[...] + lnb_ref[...]).astype(BF16)

    r = lax.broadcasted_iota(jnp.int32, (SGU_CHUNK, SGU_CHUNK), 0)
    c = lax.broadcasted_iota(jnp.int32, (SGU_CHUNK, SGU_CHUNK), 1)
    ws = [jnp.where(c <= r, ws_ref[hh], 0.0).astype(BF16) for hh in range(SGU_HEADS)]
    sbias = sbias_ref[...]
    chunks = []
    for cc in range(tm // SGU_CHUNK):
        rs = slice(cc * SGU_CHUNK, (cc + 1) * SGU_CHUNK)
        mixed = jnp.concatenate(
            [_dot(ws[hh], vn[rs, hh * SGU_DH:(hh + 1) * SGU_DH]) for hh in range(SGU_HEADS)], axis=1)
        chunks.append(mixed + sbias)
    mixed = chunks[0] if len(chunks) == 1 else jnp.concatenate(chunks, axis=0)
    return _dot((u * mixed).astype(BF16), wb_ref[...])


def _conv_branch(tail, cw_ref, wc_ref, halo_ref, tm):
    pc = tail(TAIL_CONV, TAIL_GATE)
    cb = pc[:, :CONV_W]
    xin = pc[:, CONV_W:2 * CONV_W] * pc[:, 2 * CONV_W:]
    rows = lax.broadcasted_iota(jnp.int32, (tm, CONV_W), 0)
    y = _causal_conv3(xin, halo_ref[...], cw_ref[...], rows)
    halo_ref[...] = xin[tm - SUBLANES:]
    return _dot((cb * y).astype(BF16), wc_ref[...])


def _mixer_kernel(x_ref, nrm_ref, whead_ref, walr_ref, wtail_ref, walpha_ref, balpha_ref, gnorm_ref,
                  wa_ref, lng_ref, lnb_ref, ws_ref, sbias_ref, wb_ref, cw_ref, wc_ref, wo_ref,
                  o_ref, st_ref, halo_ref, *, tm):
    @pl.when(pl.program_id(1) == 0)
    def _():
        st_ref[...] = jnp.zeros_like(st_ref)
        halo_ref[...] = jnp.zeros_like(halo_ref)

    x = x_ref[...]
    h = _rms_norm(x, nrm_ref[...]).astype(BF16)

    def tail(lo, hi):
        return _dot(h, wtail_ref[:, lo:hi])

    def gate(i):
        return jax.nn.sigmoid(tail(TAIL_GATE + i * D_MODEL, TAIL_GATE + (i + 1) * D_MODEL))

    merged = gate(0) * _gla_branch(h, whead_ref, walr_ref, walpha_ref, balpha_ref, gnorm_ref, wa_ref,
                                   st_ref, tm)
    merged = merged + gate(1) * _sgu_branch(tail, lng_ref, lnb_ref, ws_ref, sbias_ref, wb_ref, tm)
    merged = merged + gate(2) * _conv_branch(tail, cw_ref, wc_ref, halo_ref, tm)
    o_ref[...] = x + _dot(merged.astype(BF16), wo_ref[...])


def _ffn_kernel(x_ref, nrm_ref, wup_ref, cw_ref, cb_ref, wdn_ref, nf_ref,
                o_ref, act_ref, halo_ref, *, tm, final):
    @pl.when(pl.program_id(1) == 0)
    def _():
        halo_ref[...] = jnp.zeros_like(halo_ref)

    x = x_ref[...]
    h = _rms_norm(x, nrm_ref[...]).astype(BF16)
    rows = lax.broadcasted_iota(jnp.int32, (tm, FFN_BLK), 0)

    def conv_cols(lo):
        cs = slice(lo, lo + FFN_BLK)
        a = _dot(h, wup_ref[:, cs])
        y = _causal_conv3(a, halo_ref[:, cs], cw_ref[:, cs], rows) + cb_ref[:, cs]
        halo_ref[:, cs] = a[tm - SUBLANES:]
        return y

    for blk in range(D_FF // FFN_BLK):
        gate = conv_cols(blk * FFN_BLK)
        val = conv_cols(D_FF + blk * FFN_BLK)
        act_ref[:, blk * FFN_BLK:(blk + 1) * FFN_BLK] = (_silu(gate) * val).astype(BF16)

    out = x + _dot(act_ref[...], wdn_ref[...])
    if final:
        out = _rms_norm(out, nf_ref[...])
    o_ref[...] = out


def _layer_spec(w, layer):
    zeros = (0,) * (w.ndim - 1)
    return pl.BlockSpec((None,) + w.shape[1:], lambda b, j: (layer,) + zeros,
                        pipeline_mode=pl.Buffered(1))


def _tile_spec(tm):
    return pl.BlockSpec((None, tm, D_MODEL), lambda b, j: (b, j, 0))


def _compiler_params():
    return pltpu.CompilerParams(dimension_semantics=("arbitrary", "arbitrary"),
                                vmem_limit_bytes=VMEM_LIMIT_BYTES)


def _mixer_layer(x, weights, layer):
    bsz, seq, _ = x.shape
    tm = MIXER_TM
    return pl.pallas_call(
        functools.partial(_mixer_kernel, tm=tm),
        grid=(bsz, seq // tm),
        in_specs=[_tile_spec(tm)] + [_layer_spec(w, layer) for w in weights],
        out_specs=_tile_spec(tm),
        out_shape=jax.ShapeDtypeStruct(x.shape, F32),
        scratch_shapes=[pltpu.VMEM((GLA_VAL, GLA_KEY), F32),
                        pltpu.VMEM((SUBLANES, CONV_W), F32)],
        compiler_params=_compiler_params(),
        name="mixer",
    )(x, *weights)


def _ffn_layer(x, weights, layer, final):
    bsz, seq, _ = x.shape
    tm = FFN_TM
    return pl.pallas_call(
        functools.partial(_ffn_kernel, tm=tm, final=final),
        grid=(bsz, seq // tm),
        in_specs=[_tile_spec(tm)] + [_layer_spec(w, layer) for w in weights],
        out_specs=_tile_spec(tm),
        out_shape=jax.ShapeDtypeStruct(x.shape, F32),
        scratch_shapes=[pltpu.VMEM((tm, D_FF), BF16),
                        pltpu.VMEM((SUBLANES, 2 * D_FF), F32)],
        compiler_params=_compiler_params(),
        name="ffn_final" if final else "ffn",
    )(x, *weights)


def _rows(v):
    return v.reshape(v.shape[0], 1, -1).astype(F32)


def kernel(x, norm_mix, w_in, gla_w_alpha, gla_b_alpha, gla_norm, gla_w_out, sgu_ln_g, sgu_ln_b, sgu_ws, sgu_bs, sgu_w_out, conv_w, conv_w_out, w_o, norm_ffn, ffn_w_up, ffn_conv_w, ffn_conv_b, ffn_w_down, norm_final):
    assert x.shape[1] % MIXER_TM == 0 and x.shape[1] % FFN_TM == 0
    assert MIXER_TM % SGU_CHUNK == 0 and MIXER_TM % CUMSUM_BLK == 0 and D_FF % FFN_BLK == 0
    x = x.astype(F32)
    rank_pad = RANK_PAD - GLA_RANK
    w_head = w_in[:, :, :HEAD_COLS].astype(BF16)
    w_alr = jnp.pad(w_in[:, :, HEAD_COLS:HEAD_COLS + GLA_RANK], ((0, 0), (0, 0), (0, rank_pad))).astype(BF16)
    w_tail = w_in[:, :, HEAD_COLS + GLA_RANK:].astype(BF16)
    w_alpha = jnp.pad(gla_w_alpha, ((0, 0), (0, rank_pad), (0, 0))).astype(BF16)
    sbias = jnp.repeat(jnp.transpose(sgu_bs, (0, 2, 1)), SGU_DH, axis=2).astype(F32)
    mixer_weights = (
        _rows(norm_mix), w_head, w_alr, w_tail, w_alpha, _rows(gla_b_alpha),
        _rows(gla_norm), gla_w_out.astype(BF16),
        _rows(sgu_ln_g), _rows(sgu_ln_b), sgu_ws.astype(F32), sbias, sgu_w_out.astype(BF16),
        conv_w.astype(F32), conv_w_out.astype(BF16), w_o.astype(BF16))
    norm_final_rows = jnp.broadcast_to(norm_final.reshape(1, 1, -1), (DEPTH, 1, D_MODEL)).astype(F32)
    ffn_weights = (
        _rows(norm_ffn), ffn_w_up.astype(BF16), ffn_conv_w.astype(F32), _rows(ffn_conv_b),
        ffn_w_down.astype(BF16), norm_final_rows)
    for l in range(DEPTH):
        x = _mixer_layer(x, mixer_weights, l)
        x = _ffn_layer(x, ffn_weights, l, final=(l == DEPTH - 1))
    return x
```

```python
import functools

import jax
import jax.numpy as jnp
from jax import lax
from jax.experimental import pallas as pl
from jax.experimental.pallas import tpu as pltpu

F32 = jnp.float32
BF16 = jnp.bfloat16

D_MODEL = 1024
DEPTH = 4
GLA_HEADS = 4
GLA_DK = 64
GLA_DV = 128
GLA_KEY = GLA_HEADS * GLA_DK
GLA_VAL = GLA_HEADS * GLA_DV
GLA_RANK = 16
GLA_TAU = 16.0
GLA_CHUNK = 64
SGU_HEADS = 4
SGU_CHUNK = 128
SGU_W = 512
SGU_DH = SGU_W // SGU_HEADS
CONV_W = 512
N_BRANCH = 3
D_FF = 2816
EPS = 1e-6

LANES = 128
SUBLANES = 8
VMEM_LIMIT_BYTES = 56 * 1024 * 1024

RANK_PAD = LANES
HEAD_COLS = 2 * GLA_KEY + 2 * GLA_VAL
COL_Q = 0
COL_K = COL_Q + GLA_KEY
COL_V = COL_K + GLA_KEY
COL_G = COL_V + GLA_VAL
TAIL_SGU = 0
TAIL_CONV = TAIL_SGU + 2 * SGU_W
TAIL_GATE = TAIL_CONV + 3 * CONV_W
TAIL_COLS = TAIL_GATE + N_BRANCH * D_MODEL

MIXER_TM = 512
MIXER_TAIL_BLK = 256
MIXER_PRE_ITEMS = 0
MIXER_POST_BLOCKS = 2
FFN_TM = 512
FFN_BLK = 256
FFN_DOWN_GROUPS = (11,)
CUMSUM_BLK = 256


def _dot(a, b):
    return jnp.dot(a, b, preferred_element_type=F32)


def _dot_tb(a, b):
    return lax.dot_general(a, b, (((1,), (1,)), ((), ())), preferred_element_type=F32)


def _dot_ta(a, b):
    return lax.dot_general(a, b, (((0,), (0,)), ((), ())), preferred_element_type=F32)


def _rms_norm(x, g):
    ms = jnp.mean(x * x, axis=-1, keepdims=True)
    return x * lax.rsqrt(ms + EPS) * g


def _silu(x):
    return x * jax.nn.sigmoid(x)


def _causal_conv3(x, prev, w, rows):
    x1 = jnp.where(rows == 0, prev[7:8], pltpu.roll(x, 1, 0))
    x2 = jnp.where(rows == 0, prev[6:7], jnp.where(rows == 1, prev[7:8], pltpu.roll(x, 2, 0)))
    return x2 * w[0:1] + x1 * w[1:2] + x * w[2:3]


def _chunk_cumsum(la):
    t = la.shape[0]
    blk = min(t, CUMSUM_BLK)
    r = lax.broadcasted_iota(jnp.int32, (blk, blk), 0)
    c = lax.broadcasted_iota(jnp.int32, (blk, blk), 1)
    shift = GLA_CHUNK.bit_length() - 1
    tri = jnp.where(((r >> shift) == (c >> shift)) & (c <= r), 1.0, 0.0).astype(BF16)
    hi = la.astype(BF16)
    lo = (la - hi.astype(F32)).astype(BF16)
    parts = []
    for i in range(t // blk):
        sl = slice(i * blk, (i + 1) * blk)
        parts.append(_dot(tri, hi[sl]) + _dot(tri, lo[sl]))
    return parts[0] if len(parts) == 1 else jnp.concatenate(parts, axis=0)


def _gla_decay(alr, walpha_ref, balpha_ref):
    rank_lane = lax.broadcasted_iota(jnp.int32, alr.shape, 1)
    alr = jnp.where(rank_lane < GLA_RANK, alr, 0.0).astype(BF16)
    z = _dot(alr, walpha_ref[...]) + balpha_ref[...]
    log_a = (jnp.minimum(z, 0.0) - jnp.log(1.0 + jnp.exp(-jnp.abs(z)))) * (1.0 / GLA_TAU)
    return _chunk_cumsum(log_a)


def _gla_chunks(q, k, v, b, st_ref, tm, between_chunks):
    c_rows = GLA_HEADS * GLA_CHUNK
    dk_shift = GLA_DK.bit_length() - 1
    chunk_shift = GLA_CHUNK.bit_length() - 1
    lane_head = lax.broadcasted_iota(jnp.int32, (GLA_CHUNK, GLA_KEY), 1) >> dk_shift
    prow = lax.broadcasted_iota(jnp.int32, (c_rows, c_rows), 0)
    pcol = lax.broadcasted_iota(jnp.int32, (c_rows, c_rows), 1)
    causal = ((prow >> chunk_shift) == (pcol >> chunk_shift)) & (
        (pcol & (GLA_CHUNK - 1)) <= (prow & (GLA_CHUNK - 1)))

    def stack_heads(t):
        return jnp.concatenate(
            [jnp.where(lane_head == hh, t, 0.0) for hh in range(GLA_HEADS)], axis=0).astype(BF16)

    outs = []
    for c in range(tm // GLA_CHUNK):
        sl = slice(c * GLA_CHUNK, (c + 1) * GLA_CHUNK)
        bc = b[sl]
        b_last = bc[GLA_CHUNK - 1:GLA_CHUNK]
        dec = jnp.exp(b_last)
        qstack = stack_heads(q[sl] * jnp.exp(bc))
        ktstack = stack_heads(k[sl] * jnp.exp(-bc))
        kendstack = stack_heads(k[sl] * jnp.exp(b_last - bc))
        vc = v[sl].astype(BF16)
        vstack = jnp.concatenate(
            [vc[:, hh * GLA_DV:(hh + 1) * GLA_DV] for hh in range(GLA_HEADS)], axis=0)
        p = jnp.where(causal, _dot_tb(qstack, ktstack), 0.0).astype(BF16)
        st = st_ref[...]
        o_st = _dot(p, vstack) + _dot_tb(qstack, st.astype(BF16))
        st_ref[...] = st * dec + _dot_ta(vstack, kendstack)
        outs.append(jnp.concatenate(
            [o_st[hh * GLA_CHUNK:(hh + 1) * GLA_CHUNK] for hh in range(GLA_HEADS)], axis=1))
        between_chunks(c)
    return jnp.concatenate(outs, axis=0)


def _gla_out(o, g, gnorm_ref):
    gn = gnorm_ref[...]
    normed = []
    for hh in range(GLA_HEADS):
        hs = slice(hh * GLA_DV, (hh + 1) * GLA_DV)
        normed.append(_rms_norm(o[:, hs], gn[:, hs]))
    return (jnp.concatenate(normed, axis=1) * _silu(g)).astype(BF16)


def _sgu_gate_inputs(pb, lng_ref, lnb_ref):
    u = jax.nn.gelu(pb[:, :SGU_W])
    sv = jax.nn.gelu(pb[:, SGU_W:])
    mu = jnp.mean(sv, axis=-1, keepdims=True)
    xc = sv - mu
    y = xc * lax.rsqrt(jnp.mean(xc * xc, axis=-1, keepdims=True) + EPS)
    return u, (y * lng_ref[...] + lnb_ref[...]).astype(BF16)


def _sgu_mix(u, vn, ws_ref, sbias_ref, tm):
    r = lax.broadcasted_iota(jnp.int32, (SGU_CHUNK, SGU_CHUNK), 0)
    c = lax.broadcasted_iota(jnp.int32, (SGU_CHUNK, SGU_CHUNK), 1)
    ws = [jnp.where(c <= r, ws_ref[hh], 0.0).astype(BF16) for hh in range(SGU_HEADS)]
    sbias = sbias_ref[...]
    chunks = []
    for cc in range(tm // SGU_CHUNK):
        rs = slice(cc * SGU_CHUNK, (cc + 1) * SGU_CHUNK)
        mixed = jnp.concatenate(
            [_dot(ws[hh], vn[rs, hh * SGU_DH:(hh + 1) * SGU_DH]) for hh in range(SGU_HEADS)], axis=1)
        chunks.append(mixed + sbias)
    mixed = chunks[0] if len(chunks) == 1 else jnp.concatenate(chunks, axis=0)
    return (u * mixed).astype(BF16)


def _conv_gate(pc, cw_ref, halo_ref, tm):
    cb = pc[:, :CONV_W]
    xin = pc[:, CONV_W:2 * CONV_W] * pc[:, 2 * CONV_W:]
    rows = lax.broadcasted_iota(jnp.int32, (tm, CONV_W), 0)
    y = _causal_conv3(xin, halo_ref[...], cw_ref[...], rows)
    halo_ref[...] = xin[tm - SUBLANES:]
    return (cb * y).astype(BF16)


def _mixer_kernel(x_ref, nrm_ref, whead_ref, walr_ref, wtail_ref, walpha_ref, balpha_ref, gnorm_ref,
                  wa_ref, lng_ref, lnb_ref, ws_ref, sbias_ref, wb_ref, cw_ref, wc_ref, wo_ref,
                  o_ref, st_ref, halo_ref, *, tm):
    @pl.when(pl.program_id(1) == 0)
    def _():
        st_ref[...] = jnp.zeros_like(st_ref)
        halo_ref[...] = jnp.zeros_like(halo_ref)

    x = x_ref[...]
    h = _rms_norm(x, nrm_ref[...]).astype(BF16)

    alr = _dot(h, walr_ref[...])
    pa = _dot(h, whead_ref[...])
    b = _gla_decay(alr, walpha_ref, balpha_ref)
    q = pa[:, COL_Q:COL_K] * (GLA_DK ** -0.5)
    k = pa[:, COL_K:COL_V]
    v = pa[:, COL_V:COL_G]
    g = pa[:, COL_G:HEAD_COLS]

    r = {}

    def tail_block(i):
        def item():
            lo = i * MIXER_TAIL_BLK
            blk = _dot(h, wtail_ref[:, lo:lo + MIXER_TAIL_BLK])
            r[i] = jax.nn.sigmoid(blk) if lo >= TAIL_GATE else blk
        return item

    def tail_cols(lo, hi):
        return jnp.concatenate([r[i] for i in range(lo // MIXER_TAIL_BLK, hi // MIXER_TAIL_BLK)], axis=1)

    def sgu_inputs():
        r["u"], r["vn"] = _sgu_gate_inputs(tail_cols(TAIL_SGU, TAIL_CONV), lng_ref, lnb_ref)

    def conv_input():
        r["yc_in"] = _conv_gate(tail_cols(TAIL_CONV, TAIL_GATE), cw_ref, halo_ref, tm)

    def sgu_mix():
        r["yb_in"] = _sgu_mix(r["u"], r["vn"], ws_ref, sbias_ref, tm)

    def conv_out():
        r["yc"] = _dot(r["yc_in"], wc_ref[...])

    def sgu_out():
        r["yb"] = _dot(r["yb_in"], wb_ref[...])

    blocks = [tail_block(i) for i in range(TAIL_COLS // MIXER_TAIL_BLK)]
    n_sgu = TAIL_CONV // MIXER_TAIL_BLK
    n_conv = TAIL_GATE // MIXER_TAIL_BLK
    n_post = MIXER_POST_BLOCKS
    items = (blocks[:n_sgu] + [sgu_inputs] + blocks[n_sgu:n_conv] + [conv_input, sgu_mix]
             + blocks[n_conv:n_conv + 1] + [conv_out] + blocks[n_conv + 1:n_conv + 3] + [sgu_out]
             + blocks[n_conv + 3:len(blocks) - n_post])
    n_chunks = tm // GLA_CHUNK
    for item in items[:MIXER_PRE_ITEMS]:
        item()
    items = items[MIXER_PRE_ITEMS:]

    def between_chunks(c):
        for item in items[(c * len(items)) // n_chunks:((c + 1) * len(items)) // n_chunks]:
            item()

    o = _gla_chunks(q, k, v, b, st_ref, tm, between_chunks)
    for item in blocks[len(blocks) - n_post:]:
        item()
    ya = _dot(_gla_out(o, g, gnorm_ref), wa_ref[...])

    def gate(i):
        return tail_cols(TAIL_GATE + i * D_MODEL, TAIL_GATE + (i + 1) * D_MODEL)

    merged = gate(0) * ya + gate(1) * r["yb"] + gate(2) * r["yc"]
    o_ref[...] = x + _dot(merged.astype(BF16), wo_ref[...])


def _ffn_kernel(x_ref, nrm_ref, wup_ref, cw_ref, cb_ref, wdn_ref, nf_ref,
                o_ref, act_ref, halo_ref, *, tm, final):
    @pl.when(pl.program_id(1) == 0)
    def _():
        halo_ref[...] = jnp.zeros_like(halo_ref)

    x = x_ref[...]
    h = _rms_norm(x, nrm_ref[...]).astype(BF16)
    rows = lax.broadcasted_iota(jnp.int32, (tm, FFN_BLK), 0)

    def conv_cols(lo):
        cs = slice(lo, lo + FFN_BLK)
        a = _dot(h, wup_ref[:, cs])
        y = _causal_conv3(a, halo_ref[:, cs], cw_ref[:, cs], rows) + cb_ref[:, cs]
        halo_ref[:, cs] = a[tm - SUBLANES:]
        return y

    out = x
    blk = 0
    for group in FFN_DOWN_GROUPS:
        lo = blk * FFN_BLK
        for _ in range(group):
            gate = conv_cols(blk * FFN_BLK)
            val = conv_cols(D_FF + blk * FFN_BLK)
            act_ref[:, blk * FFN_BLK:(blk + 1) * FFN_BLK] = (_silu(gate) * val).astype(BF16)
            blk += 1
        hi = blk * FFN_BLK
        out = out + _dot(act_ref[:, lo:hi], wdn_ref[lo:hi, :])
    if final:
        out = _rms_norm(out, nf_ref[...])
    o_ref[...] = out


def _layer_spec(w, layer, cols=None):
    zeros = (0,) * (w.ndim - 2)
    width, col_block = (w.shape[-1], 0) if cols is None else cols
    return pl.BlockSpec((None,) + w.shape[1:-1] + (width,), lambda b, j: (layer,) + zeros + (col_block,),
                        pipeline_mode=pl.Buffered(1))


def _tile_spec(tm):
    return pl.BlockSpec((None, tm, D_MODEL), lambda b, j: (b, j, 0))


def _compiler_params(flags=None):
    return pltpu.CompilerParams(dimension_semantics=("arbitrary", "arbitrary"),
                                vmem_limit_bytes=VMEM_LIMIT_BYTES, flags=flags)


def _mixer_layer(x, weights, layer):
    bsz, seq, _ = x.shape
    tm = MIXER_TM
    return pl.pallas_call(
        functools.partial(_mixer_kernel, tm=tm),
        grid=(bsz, seq // tm),
        in_specs=[_tile_spec(tm)] + [_layer_spec(w, layer, cols) for w, cols in weights],
        out_specs=_tile_spec(tm),
        out_shape=jax.ShapeDtypeStruct(x.shape, F32),
        scratch_shapes=[pltpu.VMEM((GLA_DV, GLA_KEY), F32),
                        pltpu.VMEM((SUBLANES, CONV_W), F32)],
        compiler_params=_compiler_params(),
        name="mixer",
    )(x, *[w for w, _ in weights])


def _ffn_layer(x, weights, layer, final):
    bsz, seq, _ = x.shape
    tm = FFN_TM
    return pl.pallas_call(
        functools.partial(_ffn_kernel, tm=tm, final=final),
        grid=(bsz, seq // tm),
        in_specs=[_tile_spec(tm)] + [_layer_spec(w, layer) for w in weights],
        out_specs=_tile_spec(tm),
        out_shape=jax.ShapeDtypeStruct(x.shape, F32),
        scratch_shapes=[pltpu.VMEM((tm, D_FF), BF16),
                        pltpu.VMEM((SUBLANES, 2 * D_FF), F32)],
        compiler_params=_compiler_params(),
        name="ffn_final" if final else "ffn",
    )(x, *weights)


def _rows(v):
    return v.reshape(v.shape[0], 1, -1).astype(F32)


def kernel(x, norm_mix, w_in, gla_w_alpha, gla_b_alpha, gla_norm, gla_w_out, sgu_ln_g, sgu_ln_b, sgu_ws, sgu_bs, sgu_w_out, conv_w, conv_w_out, w_o, norm_ffn, ffn_w_up, ffn_conv_w, ffn_conv_b, ffn_w_down, norm_final):
    assert x.shape[1] % MIXER_TM == 0 and x.shape[1] % FFN_TM == 0
    assert MIXER_TM % SGU_CHUNK == 0 and MIXER_TM % CUMSUM_BLK == 0 and sum(FFN_DOWN_GROUPS) * FFN_BLK == D_FF
    assert HEAD_COLS % RANK_PAD == 0 and TAIL_COLS % MIXER_TAIL_BLK == 0 and TAIL_GATE % MIXER_TAIL_BLK == 0
    x = x.astype(F32)
    w_in_bf = w_in.astype(BF16)
    w_tail = w_in_bf[:, :, HEAD_COLS + GLA_RANK:]
    w_alpha = jnp.pad(gla_w_alpha, ((0, 0), (0, RANK_PAD - GLA_RANK), (0, 0))).astype(BF16)
    sbias = jnp.repeat(jnp.transpose(sgu_bs, (0, 2, 1)), SGU_DH, axis=2).astype(F32)
    whole = None
    mixer_weights = (
        (_rows(norm_mix), whole), (w_in_bf, (HEAD_COLS, 0)), (w_in_bf, (RANK_PAD, HEAD_COLS // RANK_PAD)),
        (w_tail, whole), (w_alpha, whole), (_rows(gla_b_alpha), whole),
        (_rows(gla_norm), whole), (gla_w_out.astype(BF16), whole),
        (_rows(sgu_ln_g), whole), (_rows(sgu_ln_b), whole), (sgu_ws.astype(F32), whole), (sbias, whole),
        (sgu_w_out.astype(BF16), whole),
        (conv_w.astype(F32), whole), (conv_w_out.astype(BF16), whole), (w_o.astype(BF16), whole))
    norm_final_rows = jnp.broadcast_to(norm_final.reshape(1, 1, -1), (DEPTH, 1, D_MODEL)).astype(F32)
    ffn_weights = (
        _rows(norm_ffn), ffn_w_up.astype(BF16), ffn_conv_w.astype(F32), _rows(ffn_conv_b),
        ffn_w_down.astype(BF16), norm_final_rows)
    for l in range(DEPTH):
        x = _mixer_layer(x, mixer_weights, l)
        x = _ffn_layer(x, ffn_weights, l, final=(l == DEPTH - 1))
    return x
```
